```python
import jax, jax.numpy as jnp
from jax import lax
import numpy as np

D_MODEL = 1024
BATCH = 8
SEQ = 2048
DEPTH = 2

MEM_LEN = 256
XA_HEADS = 4
XA_HEAD_DIM = D_MODEL // XA_HEADS
D_FF = 2816
FOURIER_WIDTH = D_MODEL // 2
FOURIER_GROUPS = 4
FOURIER_GC = FOURIER_WIDTH // FOURIER_GROUPS
CONV_WIDTH = D_MODEL // 2
CONV_GROUPS = 4
CONV_GC = CONV_WIDTH // CONV_GROUPS
CONFORMER_K = 31
EVEN_IN = FOURIER_WIDTH + 2 * CONV_WIDTH
EVEN_MIX = FOURIER_WIDTH + CONV_WIDTH
SHORT_WIDTH = D_MODEL
SHORT_K = 3
N_EVEN = (DEPTH + 1) // 2
N_ODD = DEPTH // 2
N_NORMS = 8
EPS = 1e-6
HALF = 0.5

kernel_name = "hybrid_fourier_conformer_shortconv_encoder"


def rms_norm(x, g):
    xf = x.astype(jnp.float32)
    y = xf * lax.rsqrt(jnp.mean(xf * xf, axis=-1, keepdims=True) + EPS)
    return (y * g.astype(jnp.float32)).astype(x.dtype)


def swiglu(x, w_gate, w_up, w_down):
    return (jax.nn.silu(x @ w_gate) * (x @ w_up)) @ w_down


def depthwise_conv(u, w):
    k = w.shape[0]
    pad = (k - 1) // 2
    return lax.conv_general_dilated(
        u, w[:, None, :].astype(u.dtype), window_strides=(1,),
        padding=((pad, k - 1 - pad),),
        dimension_numbers=("NWC", "WIO", "NWC"),
        feature_group_count=u.shape[-1])


def fourier_mix(u, w):
    b, s, _ = u.shape
    ug = u.reshape(b, s, FOURIER_GROUPS, FOURIER_GC).astype(jnp.float32)
    f = jnp.fft.fft2(ug, axes=(1, 3), norm="ortho").real.astype(u.dtype)
    y = jnp.einsum("bsgc,gcd->bsgd", f, w)
    return y.reshape(b, s, FOURIER_WIDTH)


def group_layer_norm(u, g, beta):
    b, s, _ = u.shape
    uf = u.reshape(b, s, CONV_GROUPS, CONV_GC).astype(jnp.float32)
    mu = jnp.mean(uf, axis=-1, keepdims=True)
    var = jnp.mean(jnp.square(uf - mu), axis=-1, keepdims=True)
    y = ((uf - mu) * lax.rsqrt(var + EPS)).reshape(b, s, CONV_WIDTH)
    return (y * g.astype(jnp.float32) + beta.astype(jnp.float32)).astype(u.dtype)


def even_mixer(h, w_in, fourier_w, dw_w, dw_b, gn_g, gn_b, w_out):
    z = h @ w_in
    u_f = z[..., :FOURIER_WIDTH]
    u_val = z[..., FOURIER_WIDTH:FOURIER_WIDTH + CONV_WIDTH]
    u_gate = z[..., FOURIER_WIDTH + CONV_WIDTH:]
    y_a = fourier_mix(u_f, fourier_w)
    c = u_val * jax.nn.sigmoid(u_gate)
    c = depthwise_conv(c, dw_w) + dw_b
    y_b = jax.nn.silu(group_layer_norm(c, gn_g, gn_b))
    return jnp.concatenate([y_a, y_b], axis=-1) @ w_out


def odd_mixer(h, w_in, conv_w, w_out):
    z = h @ w_in
    b_gate, c_gate, v = jnp.split(z, 3, axis=-1)
    y = b_gate * depthwise_conv(c_gate * v, conv_w)
    return y @ w_out


def cross_attention(h, m, wq, wkv, wo):
    b, s, _ = h.shape
    q = (h @ wq).reshape(b, s, XA_HEADS, XA_HEAD_DIM)
    kv = m @ wkv
    k, v = jnp.split(kv, 2, axis=-1)
    k = k.reshape(b, MEM_LEN, XA_HEADS, XA_HEAD_DIM)
    v = v.reshape(b, MEM_LEN, XA_HEADS, XA_HEAD_DIM)
    scores = jnp.einsum("bshd,bmhd->bhsm", q.astype(jnp.float32),
                        k.astype(jnp.float32)) * (XA_HEAD_DIM ** -0.5)
    p = jax.nn.softmax(scores, axis=-1).astype(h.dtype)
    o = jnp.einsum("bhsm,bmhd->bshd", p, v).reshape(b, s, D_MODEL)
    return o @ wo


def setup_inputs(seed: int = 0) -> dict:
    key = jax.random.key(seed)
    ks = jax.random.split(key, 24)
    f32 = jnp.float32

    def nrm(k, shape, fan_in):
        return jax.random.normal(k, shape, f32) * (fan_in ** -0.5)

    def gain(k, shape):
        return 1.0 + 0.05 * jax.random.normal(k, shape, f32)

    def bias(k, shape):
        return 0.02 * jax.random.normal(k, shape, f32)

    return {
        "x": jax.random.normal(ks[0], (BATCH, SEQ, D_MODEL), f32),
        "mem": jax.random.normal(ks[1], (BATCH, MEM_LEN, D_MODEL), f32),
        "norm_g": gain(ks[2], (DEPTH, N_NORMS, D_MODEL)),
        "mem_norm_g": gain(ks[3], (DEPTH, D_MODEL)),
        "ffn_w_gate": nrm(ks[4], (DEPTH, 2, D_MODEL, D_FF), D_MODEL),
        "ffn_w_up": nrm(ks[5], (DEPTH, 2, D_MODEL, D_FF), D_MODEL),
        "ffn_w_down": nrm(ks[6], (DEPTH, 2, D_FF, D_MODEL), D_FF),
        "xa_wq": nrm(ks[7], (DEPTH, D_MODEL, D_MODEL), D_MODEL),
        "xa_wkv": nrm(ks[8], (DEPTH, D_MODEL, 2 * D_MODEL), D_MODEL),
        "xa_wo": nrm(ks[9], (DEPTH, D_MODEL, D_MODEL), D_MODEL),
        "ev_w_in": nrm(ks[10], (N_EVEN, D_MODEL, EVEN_IN), D_MODEL),
        "ev_fourier_w": nrm(ks[11], (N_EVEN, FOURIER_GROUPS, FOURIER_GC, FOURIER_GC), FOURIER_GC),
        "ev_dw_w": nrm(ks[12], (N_EVEN, CONFORMER_K, CONV_WIDTH), CONFORMER_K),
        "ev_dw_b": bias(ks[13], (N_EVEN, CONV_WIDTH)),
        "ev_gn_g": gain(ks[14], (N_EVEN, CONV_WIDTH)),
        "ev_gn_b": bias(ks[15], (N_EVEN, CONV_WIDTH)),
        "ev_w_out": nrm(ks[16], (N_EVEN, EVEN_MIX, D_MODEL), EVEN_MIX),
        "od_w_in": nrm(ks[17], (N_ODD, D_MODEL, 3 * SHORT_WIDTH), D_MODEL),
        "od_conv_w": nrm(ks[18], (N_ODD, SHORT_K, SHORT_WIDTH), SHORT_K),
        "od_w_out": nrm(ks[19], (N_ODD, SHORT_WIDTH, D_MODEL), SHORT_WIDTH),
    }


def reference(x, mem, norm_g, mem_norm_g, ffn_w_gate, ffn_w_up, ffn_w_down,
              xa_wq, xa_wkv, xa_wo, ev_w_in, ev_fourier_w, ev_dw_w, ev_dw_b,
              ev_gn_g, ev_gn_b, ev_w_out, od_w_in, od_conv_w, od_w_out):
    for l in range(DEPTH):
        g = norm_g[l]
        h = swiglu(rms_norm(x, g[0]), ffn_w_gate[l, 0], ffn_w_up[l, 0], ffn_w_down[l, 0])
        x = x + HALF * rms_norm(h, g[1])
        h = rms_norm(x, g[2])
        i = l // 2
        if l % 2 == 0:
            h = even_mixer(h, ev_w_in[i], ev_fourier_w[i], ev_dw_w[i], ev_dw_b[i],
                           ev_gn_g[i], ev_gn_b[i], ev_w_out[i])
        else:
            h = odd_mixer(h, od_w_in[i], od_conv_w[i], od_w_out[i])
        x = x + rms_norm(h, g[3])
        m = rms_norm(mem, mem_norm_g[l])
        h = cross_attention(rms_norm(x, g[4]), m, xa_wq[l], xa_wkv[l], xa_wo[l])
        x = x + rms_norm(h, g[5])
        h = swiglu(rms_norm(x, g[6]), ffn_w_gate[l, 1], ffn_w_up[l, 1], ffn_w_down[l, 1])
        x = x + HALF * rms_norm(h, g[7])
    return x
```

```python
import functools

import jax
import jax.numpy as jnp
import numpy as np
from jax import lax
from jax.experimental import pallas as pl
from jax.experimental.pallas import tpu as pltpu

F32 = jnp.float32
BF16 = jnp.bfloat16

EPS = 1e-6
HALF = 0.5
N_NORMS = 8
XA_HEADS = 4
FOURIER_GROUPS = 4
CONV_GROUPS = 4
CONFORMER_K = 31
SHORT_K = 3

V7X_VMEM_LIMIT_BYTES = 56 * 1024 * 1024
BF16_SUBLANE_TILE = 16

TOKEN_TILE = 512
SEQ_TILE = 512
CONV_ROW_CHUNK = 64
CONV_PAD = 16
HALO = BF16_SUBLANE_TILE


def _rms(x, g):
    ms = jnp.mean(x * x, axis=-1, keepdims=True)
    return x * lax.rsqrt(ms + EPS) * g


def _dot(a, b):
    return jnp.dot(a, b, preferred_element_type=F32)


def _resident(shape):
    zeros = (0,) * len(shape)
    return pl.BlockSpec(shape, lambda *_: zeros, pipeline_mode=pl.Buffered(1))


def _params(n_grid_dims):
    return pltpu.CompilerParams(
        dimension_semantics=("arbitrary",) * n_grid_dims,
        vmem_limit_bytes=V7X_VMEM_LIMIT_BYTES)


def _ffn_kernel(x_ref, g_ref, wg_ref, wu_ref, wd_ref, o_ref, *, i_pre, i_post):
    x = x_ref[...]
    h = _rms(x, g_ref[i_pre:i_pre + 1, :]).astype(BF16)
    gate = _dot(h, wg_ref[...])
    up = _dot(h, wu_ref[...])
    act = (gate * jax.nn.sigmoid(gate) * up).astype(BF16)
    y = _dot(act, wd_ref[...])
    o_ref[...] = x + HALF * _rms(y, g_ref[i_post:i_post + 1, :])


def _ffn(x2d, gains, i_pre, i_post, wg, wu, wd):
    n, d = x2d.shape
    f = wg.shape[1]
    tile = pl.BlockSpec((TOKEN_TILE, d), lambda i: (i, 0))
    return pl.pallas_call(
        functools.partial(_ffn_kernel, i_pre=i_pre, i_post=i_post),
        out_shape=jax.ShapeDtypeStruct((n, d), F32),
        grid=(n // TOKEN_TILE,),
        in_specs=[tile, _resident((N_NORMS, d)),
                  _resident((d, f)), _resident((d, f)), _resident((f, d))],
        out_specs=tile,
        compiler_params=_params(1),
        name="ffn",
    )(x2d, gains, wg, wu, wd)


def _dft_cos_sin(n):
    k = (np.arange(n, dtype=np.int64)[:, None] * np.arange(n, dtype=np.int64)[None, :]) % n
    ang = 2.0 * np.pi * k.astype(np.float64) / n
    scale = 1.0 / np.sqrt(n)
    return (np.cos(ang) * scale).astype(np.float32), (np.sin(ang) * scale).astype(np.float32)


def _split_bf16(a):
    hi = a.astype(BF16)
    lo = (a - hi.astype(F32)).astype(BF16)
    return hi, lo


def _dot_3pass(a, b):
    a_hi, a_lo = _split_bf16(a)
    b_hi, b_lo = _split_bf16(b)
    return _dot(a_hi, b_hi) + (_dot(a_hi, b_lo) + _dot(a_lo, b_hi))


def _fold_kernel(cc_ref, sc_ref, w_ref, m_ref):
    gc = cc_ref.shape[0]
    width = gc * FOURIER_GROUPS
    m_ref[...] = jnp.zeros(m_ref.shape, m_ref.dtype)
    for g in range(FOURIER_GROUPS):
        w = w_ref[g]
        lo, hi = g * gc, (g + 1) * gc
        m_ref[lo:hi, lo:hi] = _dot_3pass(cc_ref[...], w).astype(BF16)
        m_ref[lo:hi, width + lo:width + hi] = _dot_3pass(sc_ref[...], w).astype(BF16)


def _fold_fourier(fourier_w):
    groups, gc, _ = fourier_w.shape
    cc, sc = _dft_cos_sin(gc)
    width = groups * gc
    return pl.pallas_call(
        _fold_kernel,
        out_shape=jax.ShapeDtypeStruct((width, 2 * width), BF16),
        name="fourier_fold",
    )(jnp.asarray(cc), jnp.asarray(sc), fourier_w)


def _even_in_kernel(x_ref, g_ref, win_ref, m_ref, pq_ref, c_ref, *, fw, cw):
    h = _rms(x_ref[0], g_ref[2:3, :]).astype(BF16)
    z = _dot(h, win_ref[...])
    pq = _dot(z[:, :fw].astype(BF16), m_ref[...]).astype(BF16)
    pq_ref[0, 0] = pq[:, :fw]
    pq_ref[0, 1] = pq[:, fw:]
    c_ref[0] = z[:, fw:fw + cw] * jax.nn.sigmoid(z[:, fw + cw:])


def _even_in(x, gains, w_in, m_fold, fw, cw):
    b, s, d = x.shape
    return pl.pallas_call(
        functools.partial(_even_in_kernel, fw=fw, cw=cw),
        out_shape=(jax.ShapeDtypeStruct((b, 2, s, fw), BF16),
                   jax.ShapeDtypeStruct((b, s, cw), F32)),
        grid=(b, s // SEQ_TILE),
        in_specs=[pl.BlockSpec((1, SEQ_TILE, d), lambda i, t: (i, t, 0)),
                  _resident((N_NORMS, d)),
                  _resident(w_in.shape), _resident(m_fold.shape)],
        out_specs=(pl.BlockSpec((1, 2, SEQ_TILE, fw), lambda i, t: (i, 0, t, 0)),
                   pl.BlockSpec((1, SEQ_TILE, cw), lambda i, t: (i, t, 0))),
        compiler_params=_params(2),
        name="even_in",
    )(x, gains, w_in, m_fold)


def _even_out_kernel(x_ref, g_ref, dft_ref, pq_ref, c_ref, dww_ref, dwb_ref, gng_ref,
                     gnb_ref, wout_ref, o_ref, cpad_ref, conv_ref, *, seq, gc):
    t = pl.program_id(1)
    cw = c_ref.shape[-1]

    @pl.when(t == 0)
    def _():
        zeros = jnp.zeros((CONV_PAD, cw), F32)
        cpad_ref[0:CONV_PAD, :] = zeros
        cpad_ref[CONV_PAD + seq:, :] = zeros
        cpad_ref[CONV_PAD:CONV_PAD + seq, :] = c_ref[0]

    win_rows = CONV_ROW_CHUNK + 2 * CONV_PAD

    def conv_chunk(j, carry):
        r0 = pl.multiple_of(t * SEQ_TILE + j * CONV_ROW_CHUNK, CONV_ROW_CHUNK)
        for g in range(CONV_GROUPS):
            lanes = slice(g * gc, (g + 1) * gc)
            win = cpad_ref[pl.ds(r0, win_rows), lanes]
            acc = jnp.zeros((CONV_ROW_CHUNK, gc), F32)
            for k in range(CONFORMER_K):
                acc = acc + win[k + 1:k + 1 + CONV_ROW_CHUNK, :] * dww_ref[k:k + 1, lanes]
            o0 = pl.multiple_of(j * CONV_ROW_CHUNK, CONV_ROW_CHUNK)
            conv_ref[pl.ds(o0, CONV_ROW_CHUNK), lanes] = acc + dwb_ref[:, lanes]
        return carry

    lax.fori_loop(0, SEQ_TILE // CONV_ROW_CHUNK, conv_chunk, 0)

    yb = []
    for g in range(CONV_GROUPS):
        lanes = slice(g * gc, (g + 1) * gc)
        u = conv_ref[:, lanes]
        mu = jnp.mean(u, axis=-1, keepdims=True)
        dlt = u - mu
        var = jnp.mean(dlt * dlt, axis=-1, keepdims=True)
        v = dlt * lax.rsqrt(var + EPS) * gng_ref[:, lanes] + gnb_ref[:, lanes]
        yb.append((v * jax.nn.sigmoid(v)).astype(BF16))

    ya = _dot(dft_ref[...], pq_ref[0]).astype(BF16)
    mix = jnp.concatenate([ya] + yb, axis=-1)
    y = _dot(mix, wout_ref[...])
    o_ref[0] = x_ref[0] + _rms(y, g_ref[3:4, :])


def _even_out(x, gains, dft, pq, c, dw_w, dw_b, gn_g, gn_b, w_out):
    b, s, d = x.shape
    fw = pq.shape[-1]
    cw = c.shape[-1]
    return pl.pallas_call(
        functools.partial(_even_out_kernel, seq=s, gc=cw // CONV_GROUPS),
        out_shape=jax.ShapeDtypeStruct((b, s, d), F32),
        grid=(b, s // SEQ_TILE),
        in_specs=[pl.BlockSpec((1, SEQ_TILE, d), lambda i, t: (i, t, 0)),
                  _resident((N_NORMS, d)),
                  pl.BlockSpec((SEQ_TILE, 2 * s), lambda i, t: (t, 0)),
                  pl.BlockSpec((1, 2 * s, fw), lambda i, t: (i, 0, 0)),
                  pl.BlockSpec((1, s, cw), lambda i, t: (i, 0, 0)),
                  _resident(dw_w.shape), _resident(dw_b.shape),
                  _resident(gn_g.shape), _resident(gn_b.shape),
                  _resident(w_out.shape)],
        out_specs=pl.BlockSpec((1, SEQ_TILE, d), lambda i, t: (i, t, 0)),
        scratch_shapes=[pltpu.VMEM((s + 2 * CONV_PAD, cw), F32),
                        pltpu.VMEM((SEQ_TILE, cw), F32)],
        compiler_params=_params(2),
        name="even_out",
    )(x, gains, dft, pq, c, dw_w, dw_b, gn_g, gn_b, w_out)


def _even_mixer(x, gains, w_in, fourier_w, dw_w, dw_b, gn_g, gn_b, w_out):
    b, s, d = x.shape
    fw = fourier_w.shape[0] * fourier_w.shape[1]
    cw = dw_w.shape[1]
    m_fold = _fold_fourier(fourier_w)
    cs, ss = _dft_cos_sin(s)
    dft = jnp.asarray(np.concatenate([cs, -ss], axis=1)).astype(BF16)
    pq, c = _even_in(x, gains, w_in, m_fold, fw, cw)
    return _even_out(x, gains, dft, pq.reshape(b, 2 * s, fw), c, dw_w,
                     dw_b.reshape(1, cw), gn_g.reshape(1, cw), gn_b.reshape(1, cw), w_out)


def _odd_kernel(xp_ref, x_ref, xn_ref, g_ref, win_ref, cw_ref, wout_ref, o_ref,
                h_ref, cv_ref):
    t = pl.program_id(1)
    last = pl.num_programs(1) - 1
    d = x_ref.shape[-1]
    g_pre = g_ref[2:3, :]
    x = x_ref[0]
    h_ref[0:HALO, :] = _rms(xp_ref[0], g_pre).astype(BF16)
    h_ref[HALO:HALO + SEQ_TILE, :] = _rms(x, g_pre).astype(BF16)
    h_ref[HALO + SEQ_TILE:, :] = _rms(xn_ref[0], g_pre).astype(BF16)
    z = _dot(h_ref[...], win_ref[...])
    cv = z[:, d:2 * d] * z[:, 2 * d:]
    row = lax.broadcasted_iota(jnp.int32, (SEQ_TILE + 2 * HALO, 1), 0)
    first_row = jnp.where(t > 0, 0, HALO)
    end_row = jnp.where(t < last, SEQ_TILE + 2 * HALO, SEQ_TILE + HALO)
    inside = jnp.logical_and(row >= first_row, row < end_row)
    cv_ref[...] = jnp.where(inside, cv, 0.0)
    conv = (cv_ref[HALO - 1:HALO - 1 + SEQ_TILE, :] * cw_ref[0:1, :]
            + cv_ref[HALO:HALO + SEQ_TILE, :] * cw_ref[1:2, :]
            + cv_ref[HALO + 1:HALO + 1 + SEQ_TILE, :] * cw_ref[2:3, :])
    y = (z[HALO:HALO + SEQ_TILE, :d] * conv).astype(BF16)
    out = _dot(y, wout_ref[...])
    o_ref[0] = x + _rms(out, g_ref[3:4, :])


def _odd_mixer(x, gains, w_in, conv_w, w_out):
    b, s, d = x.shape
    per_tile = SEQ_TILE // HALO
    n_halo_blocks = s // HALO
    return pl.pallas_call(
        _odd_kernel,
        out_shape=jax.ShapeDtypeStruct((b, s, d), F32),
        grid=(b, s // SEQ_TILE),
        in_specs=[pl.BlockSpec((1, HALO, d),
                               lambda i, t: (i, jnp.maximum(t * per_tile - 1, 0), 0)),
                  pl.BlockSpec((1, SEQ_TILE, d), lambda i, t: (i, t, 0)),
                  pl.BlockSpec((1, HALO, d),
                               lambda i, t: (i, jnp.minimum((t + 1) * per_tile, n_halo_blocks - 1), 0)),
                  _resident((N_NORMS, d)),
                  _resident(w_in.shape), _resident(conv_w.shape), _resident(w_out.shape)],
        out_specs=pl.BlockSpec((1, SEQ_TILE, d), lambda i, t: (i, t, 0)),
        scratch_shapes=[pltpu.VMEM((SEQ_TILE + 2 * HALO, d), BF16),
                        pltpu.VMEM((SEQ_TILE + 2 * HALO, d), F32)],
        compiler_params=_params(2),
        name="odd_mixer",
    )(x, x, x, gains, w_in, conv_w, w_out)


def _kv_kernel(m_ref, g_ref, wkv_ref, k_ref, v_ref):
    d = m_ref.shape[-1]
    m = _rms(m_ref[...], g_ref[...]).astype(BF16)
    kv = _dot(m, wkv_ref[...])
    k_ref[...] = kv[:, :d].astype(BF16)
    v_ref[...] = kv[:, d:].astype(BF16)


def _xattn_kv(mem2d, g, wkv):
    n, d = mem2d.shape
    tile = pl.BlockSpec((TOKEN_TILE, d), lambda i: (i, 0))
    return pl.pallas_call(
        _kv_kernel,
        out_shape=(jax.ShapeDtypeStruct((n, d), BF16), jax.ShapeDtypeStruct((n, d), BF16)),
        grid=(n // TOKEN_TILE,),
        in_specs=[tile, _resident((1, d)), _resident(wkv.shape)],
        out_specs=(tile, tile),
        compiler_params=_params(1),
        name="xattn_kv",
    )(mem2d, g, wkv)


def _xattn_kernel(x_ref, g_ref, wq_ref, k_ref, v_ref, wo_ref, o_ref):
    x = x_ref[0]
    d = x.shape[-1]
    hd = d // XA_HEADS
    h = _rms(x, g_ref[4:5, :]).astype(BF16)
    q = _dot(h, wq_ref[...]).astype(BF16)
    heads = []
    for i in range(XA_HEADS):
        cols = slice(i * hd, (i + 1) * hd)
        s = lax.dot_general(q[:, cols], k_ref[0, :, cols], (((1,), (1,)), ((), ())),
                            preferred_element_type=F32) * (hd ** -0.5)
        e = jnp.exp(s - jnp.max(s, axis=-1, keepdims=True))
        p = (e / jnp.sum(e, axis=-1, keepdims=True)).astype(BF16)
        heads.append(_dot(p, v_ref[0, :, cols]).astype(BF16))
    o = _dot(jnp.concatenate(heads, axis=-1), wo_ref[...])
    o_ref[0] = x + _rms(o, g_ref[5:6, :])


def _xattn(x, gains, wq, k, v, wo):
    b, s, d = x.shape
    m = k.shape[1]
    tile = pl.BlockSpec((1, SEQ_TILE, d), lambda i, t: (i, t, 0))
    mem_spec = pl.BlockSpec((1, m, d), lambda i, t: (i, 0, 0))
    return pl.pallas_call(
        _xattn_kernel,
        out_shape=jax.ShapeDtypeStruct((b, s, d), F32),
        grid=(b, s // SEQ_TILE),
        in_specs=[tile, _resident((N_NORMS, d)), _resident(wq.shape),
                  mem_spec, mem_spec, _resident(wo.shape)],
        out_specs=tile,
        compiler_params=_params(2),
        name="xattn",
    )(x, gains, wq, k, v, wo)


def kernel(x, mem, norm_g, mem_norm_g, ffn_w_gate, ffn_w_up, ffn_w_down, xa_wq, xa_wkv, xa_wo,
           ev_w_in, ev_fourier_w, ev_dw_w, ev_dw_b, ev_gn_g, ev_gn_b, ev_w_out,
           od_w_in, od_conv_w, od_w_out):
    b, s, d = x.shape
    depth = norm_g.shape[0]
    assert s % SEQ_TILE == 0 and (b * s) % TOKEN_TILE == 0 and mem.shape[1] * b % TOKEN_TILE == 0
    mem2d = mem.reshape(-1, d)
    for l in range(depth):
        g = norm_g[l]
        i = l // 2
        x = _ffn(x.reshape(b * s, d), g, 0, 1, ffn_w_gate[l, 0].astype(BF16),
                 ffn_w_up[l, 0].astype(BF16), ffn_w_down[l, 0].astype(BF16)).reshape(b, s, d)
        if l % 2 == 0:
            x = _even_mixer(x, g, ev_w_in[i].astype(BF16), ev_fourier_w[i], ev_dw_w[i], ev_dw_b[i],
                            ev_gn_g[i], ev_gn_b[i], ev_w_out[i].astype(BF16))
        else:
            x = _odd_mixer(x, g, od_w_in[i].astype(BF16), od_conv_w[i], od_w_out[i].astype(BF16))
        k, v = _xattn_kv(mem2d, mem_norm_g[l].reshape(1, d), xa_wkv[l].astype(BF16))
        x = _xattn(x, g, xa_wq[l].astype(BF16), k.reshape(b, -1, d), v.reshape(b, -1, d),
                   xa_wo[l].astype(BF16))
        x = _ffn(x.reshape(b * s, d), g, 6, 7, ffn_w_gate[l, 1].astype(BF16),
                 ffn_w_up[l, 1].astype(BF16), ffn_w_down[l, 1].astype(BF16)).reshape(b, s, d)
    return x
```
